```python
import jax, jax.numpy as jnp
from jax import lax
import numpy as np


D_MODEL = 1024
BATCH = 16
SEQ = 2048
DEPTH = 1

CHUNK = 64
D_MIX = D_MODEL
D_POOL = D_MIX // 2
POOL_WINDOWS = (2, 4, 8, 16)
N_POOL_GROUPS = len(POOL_WINDOWS)
POOL_GROUP = D_POOL // N_POOL_GROUPS
D_SGU = D_MIX - D_POOL
SGU_HEADS = 4
SGU_HEAD_DIM = D_SGU // SGU_HEADS
SGU_BLOCK = 128
D_IN = D_POOL + 2 * D_SGU
PEER_HEADS = 8
N_KEYS = 128
N_EXPERTS = N_KEYS * N_KEYS
PEER_TOPK = 16
D_QUERY = 256
D_HALF = D_QUERY // 2
PEER_TOKEN_BLOCK = 128
EPS = 1e-6

kernel_name = "hybrid_pool_sgu_peer_block"


def rmsnorm(x, g):
    xf = x.astype(jnp.float32)
    y = xf * lax.rsqrt(jnp.mean(xf * xf, axis=-1, keepdims=True) + EPS)
    return (y * g.astype(jnp.float32)).astype(x.dtype)


def layernorm(x, g):
    xf = x.astype(jnp.float32)
    mu = jnp.mean(xf, axis=-1, keepdims=True)
    var = jnp.mean(jnp.square(xf - mu), axis=-1, keepdims=True)
    return ((xf - mu) * lax.rsqrt(var + EPS) * g.astype(jnp.float32)).astype(x.dtype)


def pool_mixer(a, pool_w, pool_scale):
    B, S, _ = a.shape
    af = a.astype(jnp.float32).reshape(B, S, N_POOL_GROUPS, POOL_GROUP)
    cs = jnp.cumsum(af, axis=1)
    pos = jnp.arange(S, dtype=jnp.float32)[None, :, None]
    outs = []
    for g, w in enumerate(POOL_WINDOWS):
        csg = cs[:, :, g]
        lagged = jnp.pad(csg, ((0, 0), (w, 0), (0, 0)))[:, :S]
        count = jnp.minimum(pos + 1.0, float(w))
        outs.append((csg - lagged) / count - af[:, :, g])
    d = jnp.stack(outs, axis=2).astype(a.dtype)
    y = jnp.einsum('bsgc,gcd->bsgd', d, pool_w)
    return y.reshape(B, S, D_POOL) * pool_scale


def sgu_mixer(uv, sgu_norm_g, sgu_w, sgu_b):
    B, S, _ = uv.shape
    uv = jax.nn.gelu(uv)
    u, v = uv[..., :D_SGU], uv[..., D_SGU:]
    v = layernorm(v, sgu_norm_g)
    ch = jnp.arange(SGU_BLOCK) // CHUNK
    mask = ch[None, :] <= ch[:, None]
    ws = jnp.where(mask[None], sgu_w, jnp.zeros_like(sgu_w))
    vb = v.reshape(B, S // SGU_BLOCK, SGU_BLOCK, SGU_HEADS, SGU_HEAD_DIM)
    gate = jnp.einsum('hij,bnjhc->bnihc', ws, vb) + jnp.transpose(sgu_b)[None, None, :, :, None]
    return u * gate.reshape(B, S, D_SGU)


def peer_ffn(h, wq, keys, u_tab, v_tab):
    B, S, D = h.shape
    xt = h.reshape((B * S) // PEER_TOKEN_BLOCK, PEER_TOKEN_BLOCK, D)

    def block(xb):
        T = xb.shape[0]
        q = (xb @ wq).reshape(T, PEER_HEADS, 2, D_HALF)
        s = jnp.einsum('thpd,phkd->thpk', q, keys).astype(jnp.float32)
        s_top, i_top = lax.top_k(s, PEER_TOPK)
        cand_s = s_top[:, :, 0, :, None] + s_top[:, :, 1, None, :]
        cand_i = i_top[:, :, 0, :, None] * N_KEYS + i_top[:, :, 1, None, :]
        cand_s = cand_s.reshape(T, PEER_HEADS, PEER_TOPK * PEER_TOPK)
        cand_i = cand_i.reshape(T, PEER_HEADS, PEER_TOPK * PEER_TOPK)
        best_s, best_pos = lax.top_k(cand_s, PEER_TOPK)
        idx = jnp.take_along_axis(cand_i, best_pos, axis=-1)
        gate = jax.nn.softmax(best_s, axis=-1)
        u = u_tab[idx]
        v = v_tab[idx]
        act = jax.nn.gelu(jnp.einsum('td,thkd->thk', xb, u).astype(jnp.float32))
        return jnp.einsum('thk,thkd->td', (gate * act).astype(xb.dtype), v)

    return lax.map(block, xt).reshape(B, S, D)


def setup_inputs(seed: int = 0) -> dict:
    key = jax.random.key(seed)
    ks = jax.random.split(key, 16)
    f32 = jnp.float32
    L, D = DEPTH, D_MODEL
    x = jax.random.normal(ks[0], (BATCH, SEQ, D), f32)
    norm1_g = 1.0 + 0.02 * jax.random.normal(ks[1], (L, D), f32)
    w_in = jax.random.normal(ks[2], (L, D, D_IN), f32) * D ** -0.5
    pool_w = jax.random.normal(ks[3], (L, N_POOL_GROUPS, POOL_GROUP, POOL_GROUP), f32) * POOL_GROUP ** -0.5
    pool_scale = 1.0 + 0.02 * jax.random.normal(ks[4], (L, D_POOL), f32)
    sgu_norm_g = 1.0 + 0.02 * jax.random.normal(ks[5], (L, D_SGU), f32)
    sgu_w = jax.random.normal(ks[6], (L, SGU_HEADS, SGU_BLOCK, SGU_BLOCK), f32) * SGU_BLOCK ** -0.5
    sgu_b = 1.0 + 0.02 * jax.random.normal(ks[7], (L, SGU_HEADS, SGU_BLOCK), f32)
    w_out = jax.random.normal(ks[8], (L, D_MIX, D), f32) * D_MIX ** -0.5
    norm2_g = 1.0 + 0.02 * jax.random.normal(ks[9], (L, D), f32)
    peer_wq = jax.random.normal(ks[10], (L, D, PEER_HEADS * D_QUERY), f32) * D ** -0.5
    peer_keys = jax.random.normal(ks[11], (L, 2, PEER_HEADS, N_KEYS, D_HALF), f32) * D_HALF ** -0.5
    peer_u = jax.random.normal(ks[12], (L, N_EXPERTS, D), f32) * D ** -0.5
    peer_v = jax.random.normal(ks[13], (L, N_EXPERTS, D), f32) * PEER_HEADS ** -0.5
    final_g = 1.0 + 0.02 * jax.random.normal(ks[14], (D,), f32)
    return {"x": x, "norm1_g": norm1_g, "w_in": w_in, "pool_w": pool_w,
            "pool_scale": pool_scale, "sgu_norm_g": sgu_norm_g, "sgu_w": sgu_w,
            "sgu_b": sgu_b, "w_out": w_out, "norm2_g": norm2_g, "peer_wq": peer_wq,
            "peer_keys": peer_keys, "peer_u": peer_u, "peer_v": peer_v,
            "final_g": final_g}


def reference(x, norm1_g, w_in, pool_w, pool_scale, sgu_norm_g, sgu_w, sgu_b,
              w_out, norm2_g, peer_wq, peer_keys, peer_u, peer_v, final_g):
    h = x
    for l in range(DEPTH):
        z = rmsnorm(h, norm1_g[l]) @ w_in[l]
        a = pool_mixer(z[..., :D_POOL], pool_w[l], pool_scale[l])
        b = sgu_mixer(z[..., D_POOL:], sgu_norm_g[l], sgu_w[l], sgu_b[l])
        h = h + jnp.concatenate([a, b], axis=-1) @ w_out[l]
        h = h + peer_ffn(rmsnorm(h, norm2_g[l]), peer_wq[l], peer_keys[l], peer_u[l], peer_v[l])
    return rmsnorm(h, final_g)
```

```python
import functools

import jax
import jax.numpy as jnp
from jax import lax
from jax.experimental import pallas as pl
from jax.experimental.pallas import tpu as pltpu

F32 = jnp.float32
BF16 = jnp.bfloat16
I32 = jnp.int32

EPS = 1e-6
CHUNK = 64
POOL_WINDOWS = (2, 4, 8, 16)
POOL_HALO = 16
LANES = 128
SUBLANES = 8
SGU_BLOCK = 128
PEER_HEADS = 8
N_KEYS = 128
TOPK = 16
HALF_ROWS = 4
HI_MASK = -65536

MIX_TOKENS = 256
ROUTE_TOKENS = 256
EXPERT_TOKENS = 128
VMEM_LIMIT = 56 * 1024 * 1024


def _gelu_tanh(x):
    c = 0.7978845608028654
    return 0.5 * x * (1.0 + jnp.tanh(c * (x + 0.044715 * (x * x * x))))


def _mixer_kernel(x_ref, g1_ref, win_ref, poolw_ref, pscale_ref, sgug_ref,
                  ws_ref, sgub_ref, wout_ref, o_ref, carry_ref, *, ts):
    s_idx = pl.program_id(1)

    @pl.when(s_idx == 0)
    def _():
        carry_ref[...] = jnp.zeros_like(carry_ref)

    x = x_ref[0]
    d_model = x.shape[-1]
    ms = jnp.sum(x * x, axis=-1, keepdims=True) / d_model
    xn = x * lax.rsqrt(ms + EPS) * g1_ref[...]
    z = jnp.dot(xn.astype(BF16), win_ref[...], preferred_element_type=F32)

    d_pool = LANES * len(POOL_WINDOWS)
    a = z[:, :d_pool]
    hist = jnp.concatenate([carry_ref[...], a], axis=0)
    carry_ref[...] = a[ts - POOL_HALO:, :]
    pos = s_idx * ts + lax.broadcasted_iota(I32, (ts, 1), 0)
    mixed = []
    for g, w in enumerate(POOL_WINDOWS):
        cols = slice(g * LANES, (g + 1) * LANES)
        sm = hist[:, cols]
        sh = 1
        while sh < w:
            sm = sm + pltpu.roll(sm, sh, axis=0)
            sh *= 2
        cnt = jnp.minimum(pos + 1, w).astype(F32)
        dg = sm[POOL_HALO:, :] / cnt - a[:, cols]
        yg = jnp.dot(dg.astype(BF16), poolw_ref[g], preferred_element_type=F32)
        mixed.append(yg * pscale_ref[:, cols])

    d_sgu = (z.shape[-1] - d_pool) // 2
    uv = _gelu_tanh(z[:, d_pool:])
    u = uv[:, :d_sgu]
    v = uv[:, d_sgu:]
    mu = jnp.sum(v, axis=-1, keepdims=True) / d_sgu
    vc = v - mu
    var = jnp.sum(vc * vc, axis=-1, keepdims=True) / d_sgu
    vn = (vc * lax.rsqrt(var + EPS) * sgug_ref[...]).astype(BF16)
    ri = lax.broadcasted_iota(I32, (SGU_BLOCK, SGU_BLOCK), 0) // CHUNK
    ci = lax.broadcasted_iota(I32, (SGU_BLOCK, SGU_BLOCK), 1) // CHUNK
    causal = ci <= ri
    heads = d_sgu // LANES
    for h in range(heads):
        cols = slice(h * LANES, (h + 1) * LANES)
        wsh = jnp.where(causal, ws_ref[h], 0.0).astype(BF16)
        bias = sgub_ref[:, h:h + 1]
        rows = []
        for n in range(ts // SGU_BLOCK):
            rs = slice(n * SGU_BLOCK, (n + 1) * SGU_BLOCK)
            gate = jnp.dot(wsh, vn[rs, cols], preferred_element_type=F32) + bias
            rows.append(u[rs, cols] * gate)
        mixed.append(jnp.concatenate(rows, axis=0))

    mix = jnp.concatenate(mixed, axis=-1).astype(BF16)
    o_ref[0] = x + jnp.dot(mix, wout_ref[...], preferred_element_type=F32)


def _mixer(x, g1, w_in, pool_w, pool_scale, sgu_g, sgu_w, sgu_b, w_out):
    b, s, d = x.shape
    ts = min(MIX_TOKENS, s)
    assert s % ts == 0 and ts % SGU_BLOCK == 0
    d_in = w_in.shape[1]
    d_pool = pool_scale.shape[0]
    d_sgu = sgu_g.shape[0]
    const = lambda *shape: pl.BlockSpec(shape, lambda i, j: (0,) * len(shape))
    return pl.pallas_call(
        functools.partial(_mixer_kernel, ts=ts),
        grid=(b, s // ts),
        in_specs=[
            pl.BlockSpec((1, ts, d), lambda i, j: (i, j, 0)),
            const(1, d),
            const(d, d_in),
            const(len(POOL_WINDOWS), LANES, LANES),
            const(1, d_pool),
            const(1, d_sgu),
            const(d_sgu // LANES, SGU_BLOCK, SGU_BLOCK),
            const(SGU_BLOCK, d_sgu // LANES),
            const(d, d),
        ],
        out_specs=pl.BlockSpec((1, ts, d), lambda i, j: (i, j, 0)),
        out_shape=jax.ShapeDtypeStruct((b, s, d), F32),
        scratch_shapes=[pltpu.VMEM((POOL_HALO, d_pool), F32)],
        compiler_params=pltpu.CompilerParams(
            dimension_semantics=("arbitrary", "arbitrary"),
            vmem_limit_bytes=VMEM_LIMIT),
        name="mixer",
    )(x, g1.reshape(1, d), w_in.astype(BF16), pool_w.astype(BF16),
      pool_scale.reshape(1, d_pool), sgu_g.reshape(1, d_sgu), sgu_w,
      sgu_b.T, w_out.astype(BF16))


def _top16_rows(s):
    n, ts = s.shape
    row = lax.broadcasted_iota(I32, (n, ts), 0).astype(F32)
    slot = lax.broadcasted_iota(I32, (TOPK, ts), 0)
    vals = jnp.zeros((TOPK, ts), F32)
    ids = jnp.zeros((TOPK, ts), F32)
    for r in range(TOPK):
        m = jnp.max(s, axis=0, keepdims=True)
        am = jnp.min(jnp.where(s == m, row, float(n)), axis=0, keepdims=True)
        vals = jnp.where(slot == r, m, vals)
        ids = jnp.where(slot == r, am, ids)
        s = jnp.where(row == am, -jnp.inf, s)
    return vals, ids


def _route_kernel(h_ref, g2_ref, wq_ref, keys_ref, xn_ref, idx_ref, gate_ref,
                  hb_ref):
    head = pl.program_id(1)

    @pl.when(head == 0)
    def _():
        hh = h_ref[...]
        ms = jnp.sum(hh * hh, axis=-1, keepdims=True) / hh.shape[-1]
        hn = hh * lax.rsqrt(ms + EPS) * g2_ref[...]
        xn_ref[...] = hn
        hb_ref[...] = hn.astype(BF16)

    q = jnp.dot(hb_ref[...], wq_ref[...], preferred_element_type=F32)
    ts = q.shape[0]
    tops = []
    for p in range(2):
        qp = q[:, p * LANES:(p + 1) * LANES].astype(BF16)
        st = lax.dot_general(keys_ref[p, 0], qp, (((1,), (1,)), ((), ())),
                             preferred_element_type=F32)
        tops.append(_top16_rows(st))
    (s1, i1), (s2, i2) = tops

    sub = lax.broadcasted_iota(I32, (TOPK, ts), 0).astype(F32)
    cand, cpos, cid = [], [], []
    for i in range(TOPK):
        cand.append(s1[i:i + 1, :] + s2)
        cpos.append(sub + float(i * TOPK))
        cid.append(i1[i:i + 1, :] * float(N_KEYS) + i2)
    slot = lax.broadcasted_iota(I32, (TOPK, ts), 0)
    best = jnp.zeros((TOPK, ts), F32)
    bid = jnp.zeros((TOPK, ts), F32)
    for r in range(TOPK):
        m16 = cand[0]
        for i in range(1, TOPK):
            m16 = jnp.maximum(m16, cand[i])
        m = jnp.max(m16, axis=0, keepdims=True)
        p16 = jnp.where(cand[0] == m, cpos[0], float(TOPK * TOPK))
        for i in range(1, TOPK):
            p16 = jnp.minimum(p16, jnp.where(cand[i] == m, cpos[i], float(TOPK * TOPK)))
        am = jnp.min(p16, axis=0, keepdims=True)
        e16 = jnp.zeros((TOPK, ts), F32)
        for i in range(TOPK):
            hit = cpos[i] == am
            e16 = jnp.maximum(e16, jnp.where(hit, cid[i], 0.0))
            cand[i] = jnp.where(hit, -jnp.inf, cand[i])
        eid = jnp.max(e16, axis=0, keepdims=True)
        best = jnp.where(slot == r, m, best)
        bid = jnp.where(slot == r, eid, bid)

    e = jnp.exp(best - jnp.max(best, axis=0, keepdims=True))
    gate_ref[...] = e / jnp.sum(e, axis=0, keepdims=True)
    idx_ref[...] = bid.astype(I32)


def _route(h2d, g2, wq, keys):
    t, d = h2d.shape
    ts = min(ROUTE_TOKENS, t)
    assert t % ts == 0
    dq = wq.shape[1] // PEER_HEADS
    return pl.pallas_call(
        _route_kernel,
        grid=(t // ts, PEER_HEADS),
        in_specs=[
            pl.BlockSpec((ts, d), lambda i, h: (i, 0)),
            pl.BlockSpec((1, d), lambda i, h: (0, 0)),
            pl.BlockSpec((d, dq), lambda i, h: (0, h)),
            pl.BlockSpec((2, 1, N_KEYS, dq // 2), lambda i, h: (0, h, 0, 0)),
        ],
        out_specs=[
            pl.BlockSpec((ts, d), lambda i, h: (i, 0)),
            pl.BlockSpec((TOPK, ts), lambda i, h: (h, i)),
            pl.BlockSpec((TOPK, ts), lambda i, h: (h, i)),
        ],
        out_shape=[
            jax.ShapeDtypeStruct((t, d), F32),
            jax.ShapeDtypeStruct((PEER_HEADS * TOPK, t), I32),
            jax.ShapeDtypeStruct((PEER_HEADS * TOPK, t), F32),
        ],
        scratch_shapes=[pltpu.VMEM((ts, d), BF16)],
        compiler_params=pltpu.CompilerParams(
            dimension_semantics=("arbitrary", "arbitrary"),
            vmem_limit_bytes=VMEM_LIMIT),
        name="route",
    )(h2d, g2.reshape(1, d), wq.astype(BF16), keys.astype(BF16))


def _pack_table(tab):
    n, d = tab.shape
    assert d == 2 * HALF_ROWS * LANES
    tb = lax.bitcast_convert_type(tab.astype(BF16), jnp.uint16).astype(jnp.uint32)
    word = tb[:, :d // 2] | (tb[:, d // 2:] << 16)
    return lax.bitcast_convert_type(word, I32).reshape(n, HALF_ROWS, LANES)


def _unpack(word):
    lo = lax.bitcast_convert_type(word << 16, F32)
    hi = lax.bitcast_convert_type(word & HI_MASK, F32)
    return lo, hi


def _expert_in_kernel(idx_ref, x_ref, tab_ref, gate_ref, w_ref, prod_ref, *, tb, nsel):
    kk = lax.broadcasted_iota(I32, (nsel, nsel * HALF_ROWS), 0)
    jj = lax.broadcasted_iota(I32, (nsel, nsel * HALF_ROWS), 1)
    sel = jnp.where(jj // HALF_ROWS == kk, 1.0, 0.0).astype(BF16)
    lane = lax.broadcasted_iota(I32, (nsel, tb), 1)

    def token(t, acc):
        xlo = x_ref[t, 0:HALF_ROWS, :]
        xhi = x_ref[t, HALF_ROWS:2 * HALF_ROWS, :]
        for k in range(nsel):
            lo, hi = _unpack(tab_ref[idx_ref[t, k]])
            prod_ref[k * HALF_ROWS:(k + 1) * HALF_ROWS, :] = lo * xlo + hi * xhi
        part = jnp.dot(sel, prod_ref[...].astype(BF16), preferred_element_type=F32)
        col = jnp.sum(part, axis=1, keepdims=True)
        return jnp.where(lane == t, col, acc)

    dots = lax.fori_loop(0, tb, token, jnp.zeros((nsel, tb), F32))
    w_ref[...] = gate_ref[...] * _gelu_tanh(dots)


def _expert_in(idx, x3, tab, gate_t):
    t, nsel = idx.shape
    tb = min(EXPERT_TOKENS, t)
    assert t % tb == 0
    return pl.pallas_call(
        functools.partial(_expert_in_kernel, tb=tb, nsel=nsel),
        grid=(t // tb,),
        in_specs=[
            pl.BlockSpec((tb, nsel), lambda i: (i, 0), memory_space=pltpu.SMEM),
            pl.BlockSpec((tb, SUBLANES, LANES), lambda i: (i, 0, 0)),
            pl.BlockSpec(tab.shape, lambda i: (0, 0, 0), pipeline_mode=pl.Buffered(1)),
            pl.BlockSpec((nsel, tb), lambda i: (0, i)),
        ],
        out_specs=pl.BlockSpec((nsel, tb), lambda i: (0, i)),
        out_shape=jax.ShapeDtypeStruct((nsel, t), F32),
        scratch_shapes=[pltpu.VMEM((nsel * HALF_ROWS, LANES), F32)],
        compiler_params=pltpu.CompilerParams(
            dimension_semantics=("arbitrary",),
            vmem_limit_bytes=VMEM_LIMIT),
        name="expert_in",
    )(idx, x3, tab, gate_t)


def _expert_out_kernel(idx_ref, w_ref, tab_ref, h_ref, g_ref, o_ref, peer_ref, *, tb, nsel):
    chains = 4

    def token(t, carry):
        acc_lo = [jnp.zeros((HALF_ROWS, LANES), F32) for _ in range(chains)]
        acc_hi = [jnp.zeros((HALF_ROWS, LANES), F32) for _ in range(chains)]
        for k in range(nsel):
            lo, hi = _unpack(tab_ref[idx_ref[t, k]])
            wk = w_ref[t, k]
            acc_lo[k % chains] = acc_lo[k % chains] + wk * lo
            acc_hi[k % chains] = acc_hi[k % chains] + wk * hi
        peer_ref[t, 0:HALF_ROWS, :] = (acc_lo[0] + acc_lo[1]) + (acc_lo[2] + acc_lo[3])
        peer_ref[t, HALF_ROWS:2 * HALF_ROWS, :] = (acc_hi[0] + acc_hi[1]) + (acc_hi[2] + acc_hi[3])
        return carry

    lax.fori_loop(0, tb, token, 0)
    hh = h_ref[...] + peer_ref[...]
    sq = jnp.sum(hh * hh, axis=2, keepdims=True)
    ms = jnp.sum(sq, axis=1, keepdims=True) / (SUBLANES * LANES)
    o_ref[...] = hh * lax.rsqrt(ms + EPS) * g_ref[...]


def _expert_out(idx, w, tab, h3, g):
    t, nsel = idx.shape
    tb = min(EXPERT_TOKENS, t)
    assert t % tb == 0
    return pl.pallas_call(
        functools.partial(_expert_out_kernel, tb=tb, nsel=nsel),
        grid=(t // tb,),
        in_specs=[
            pl.BlockSpec((tb, nsel), lambda i: (i, 0), memory_space=pltpu.SMEM),
            pl.BlockSpec((tb, nsel), lambda i: (i, 0), memory_space=pltpu.SMEM),
            pl.BlockSpec(tab.shape, lambda i: (0, 0, 0), pipeline_mode=pl.Buffered(1)),
            pl.BlockSpec((tb, SUBLANES, LANES), lambda i: (i, 0, 0)),
            pl.BlockSpec((1, SUBLANES, LANES), lambda i: (0, 0, 0)),
        ],
        out_specs=pl.BlockSpec((tb, SUBLANES, LANES), lambda i: (i, 0, 0)),
        out_shape=jax.ShapeDtypeStruct((t, SUBLANES, LANES), F32),
        scratch_shapes=[pltpu.VMEM((tb, SUBLANES, LANES), F32)],
        compiler_params=pltpu.CompilerParams(
            dimension_semantics=("arbitrary",),
            vmem_limit_bytes=VMEM_LIMIT),
        name="expert_out",
    )(idx, w, tab, h3, g.reshape(1, SUBLANES, LANES))


def kernel(x, norm1_g, w_in, pool_w, pool_scale, sgu_norm_g, sgu_w, sgu_b, w_out,
           norm2_g, peer_wq, peer_keys, peer_u, peer_v, final_g):
    b, s, d = x.shape
    assert norm1_g.shape[0] == 1 and d == SUBLANES * LANES
    h = _mixer(x, norm1_g[0], w_in[0], pool_w[0], pool_scale[0], sgu_norm_g[0],
               sgu_w[0], sgu_b[0], w_out[0])
    h2d = h.reshape(b * s, d)
    xn, idx_t, gate_t = _route(h2d, norm2_g[0], peer_wq[0], peer_keys[0])
    idx = idx_t.T
    w_t = _expert_in(idx, xn.reshape(b * s, SUBLANES, LANES),
                     _pack_table(peer_u[0]), gate_t)
    out = _expert_out(idx, w_t.T, _pack_table(peer_v[0]),
                      h2d.reshape(b * s, SUBLANES, LANES), final_g)
    return out.reshape(b, s, d)
```

```python
import functools

import jax
import jax.numpy as jnp
from jax import lax
from jax.experimental import pallas as pl
from jax.experimental.pallas import tpu as pltpu

F32 = jnp.float32
BF16 = jnp.bfloat16
I32 = jnp.int32

EPS = 1e-6
CHUNK = 64
POOL_WINDOWS = (2, 4, 8, 16)
POOL_HALO = 16
LANES = 128
SUBLANES = 8
SGU_BLOCK = 128
PEER_HEADS = 8
N_KEYS = 128
TOPK = 16
HALF_ROWS = 4

MIX_TOKENS = 256
ROUTE_TOKENS = 256
EXPERT_TOKENS = 128
VMEM_LIMIT = 56 * 1024 * 1024


def _gelu_tanh(x):
    c = 0.7978845608028654
    return 0.5 * x * (1.0 + jnp.tanh(c * (x + 0.044715 * (x * x * x))))


def _mixer_kernel(x_ref, g1_ref, win_ref, poolw_ref, pscale_ref, sgug_ref,
                  ws_ref, sgub_ref, wout_ref, o_ref, carry_ref, *, ts):
    s_idx = pl.program_id(1)

    @pl.when(s_idx == 0)
    def _():
        carry_ref[...] = jnp.zeros_like(carry_ref)

    x = x_ref[0]
    d_model = x.shape[-1]
    ms = jnp.sum(x * x, axis=-1, keepdims=True) / d_model
    xn = x * lax.rsqrt(ms + EPS) * g1_ref[...]
    z = jnp.dot(xn.astype(BF16), win_ref[...], preferred_element_type=F32)

    d_pool = LANES * len(POOL_WINDOWS)
    a = z[:, :d_pool]
    hist = jnp.concatenate([carry_ref[...], a], axis=0)
    carry_ref[...] = a[ts - POOL_HALO:, :]
    pos = s_idx * ts + lax.broadcasted_iota(I32, (ts, 1), 0)
    mixed = []
    for g, w in enumerate(POOL_WINDOWS):
        cols = slice(g * LANES, (g + 1) * LANES)
        sm = hist[:, cols]
        sh = 1
        while sh < w:
            sm = sm + pltpu.roll(sm, sh, axis=0)
            sh *= 2
        cnt = jnp.minimum(pos + 1, w).astype(F32)
        dg = sm[POOL_HALO:, :] / cnt - a[:, cols]
        yg = jnp.dot(dg.astype(BF16), poolw_ref[g], preferred_element_type=F32)
        mixed.append(yg * pscale_ref[:, cols])

    d_sgu = (z.shape[-1] - d_pool) // 2
    uv = _gelu_tanh(z[:, d_pool:])
    u = uv[:, :d_sgu]
    v = uv[:, d_sgu:]
    mu = jnp.sum(v, axis=-1, keepdims=True) / d_sgu
    vc = v - mu
    var = jnp.sum(vc * vc, axis=-1, keepdims=True) / d_sgu
    vn = (vc * lax.rsqrt(var + EPS) * sgug_ref[...]).astype(BF16)
    ri = lax.broadcasted_iota(I32, (SGU_BLOCK, SGU_BLOCK), 0) // CHUNK
    ci = lax.broadcasted_iota(I32, (SGU_BLOCK, SGU_BLOCK), 1) // CHUNK
    causal = ci <= ri
    heads = d_sgu // LANES
    for h in range(heads):
        cols = slice(h * LANES, (h + 1) * LANES)
        wsh = jnp.where(causal, ws_ref[h], 0.0).astype(BF16)
        bias = sgub_ref[:, h:h + 1]
        rows = []
        for n in range(ts // SGU_BLOCK):
            rs = slice(n * SGU_BLOCK, (n + 1) * SGU_BLOCK)
            gate = jnp.dot(wsh, vn[rs, cols], preferred_element_type=F32) + bias
            rows.append(u[rs, cols] * gate)
        mixed.append(jnp.concatenate(rows, axis=0))

    mix = jnp.concatenate(mixed, axis=-1).astype(BF16)
    o_ref[0] = x + jnp.dot(mix, wout_ref[...], preferred_element_type=F32)


def _mixer(x, g1, w_in, pool_w, pool_scale, sgu_g, sgu_w, sgu_b, w_out):
    b, s, d = x.shape
    ts = min(MIX_TOKENS, s)
    assert s % ts == 0 and ts % SGU_BLOCK == 0
    d_in = w_in.shape[1]
    d_pool = pool_scale.shape[0]
    d_sgu = sgu_g.shape[0]
    const = lambda *shape: pl.BlockSpec(shape, lambda i, j: (0,) * len(shape))
    return pl.pallas_call(
        functools.partial(_mixer_kernel, ts=ts),
        grid=(b, s // ts),
        in_specs=[
            pl.BlockSpec((1, ts, d), lambda i, j: (i, j, 0)),
            const(1, d),
            const(d, d_in),
            const(len(POOL_WINDOWS), LANES, LANES),
            const(1, d_pool),
            const(1, d_sgu),
            const(d_sgu // LANES, SGU_BLOCK, SGU_BLOCK),
            const(SGU_BLOCK, d_sgu // LANES),
            const(d, d),
        ],
        out_specs=pl.BlockSpec((1, ts, d), lambda i, j: (i, j, 0)),
        out_shape=jax.ShapeDtypeStruct((b, s, d), F32),
        scratch_shapes=[pltpu.VMEM((POOL_HALO, d_pool), F32)],
        compiler_params=pltpu.CompilerParams(
            dimension_semantics=("arbitrary", "arbitrary"),
            vmem_limit_bytes=VMEM_LIMIT),
        name="mixer",
    )(x, g1.reshape(1, d), w_in.astype(BF16), pool_w.astype(BF16),
      pool_scale.reshape(1, d_pool), sgu_g.reshape(1, d_sgu), sgu_w,
      sgu_b.T, w_out.astype(BF16))


def _top16_rows(s):
    n, ts = s.shape
    row = lax.broadcasted_iota(I32, (n, ts), 0).astype(F32)
    slot = lax.broadcasted_iota(I32, (TOPK, ts), 0)
    vals = jnp.zeros((TOPK, ts), F32)
    ids = jnp.zeros((TOPK, ts), F32)
    for r in range(TOPK):
        m = jnp.max(s, axis=0, keepdims=True)
        am = jnp.min(jnp.where(s == m, row, float(n)), axis=0, keepdims=True)
        vals = jnp.where(slot == r, m, vals)
        ids = jnp.where(slot == r, am, ids)
        s = jnp.where(row == am, -jnp.inf, s)
    return vals, ids


def _route_kernel(h_ref, g2_ref, wq_ref, keys_ref, xn_ref, idx_ref, gate_ref,
                  hb_ref):
    head = pl.program_id(1)

    @pl.when(head == 0)
    def _():
        hh = h_ref[...]
        ms = jnp.sum(hh * hh, axis=-1, keepdims=True) / hh.shape[-1]
        hn = hh * lax.rsqrt(ms + EPS) * g2_ref[...]
        xn_ref[...] = hn
        hb_ref[...] = hn.astype(BF16)

    q = jnp.dot(hb_ref[...], wq_ref[...], preferred_element_type=F32)
    ts = q.shape[0]
    tops = []
    for p in range(2):
        qp = q[:, p * LANES:(p + 1) * LANES].astype(BF16)
        st = lax.dot_general(keys_ref[p, 0], qp, (((1,), (1,)), ((), ())),
                             preferred_element_type=F32)
        tops.append(_top16_rows(st))
    (s1, i1), (s2, i2) = tops

    sub = lax.broadcasted_iota(I32, (TOPK, ts), 0).astype(F32)
    cand, cpos, cid = [], [], []
    for i in range(TOPK):
        cand.append(s1[i:i + 1, :] + s2)
        cpos.append(sub + float(i * TOPK))
        cid.append(i1[i:i + 1, :] * float(N_KEYS) + i2)
    slot = lax.broadcasted_iota(I32, (TOPK, ts), 0)
    best = jnp.zeros((TOPK, ts), F32)
    bid = jnp.zeros((TOPK, ts), F32)
    for r in range(TOPK):
        m16 = cand[0]
        for i in range(1, TOPK):
            m16 = jnp.maximum(m16, cand[i])
        m = jnp.max(m16, axis=0, keepdims=True)
        p16 = jnp.where(cand[0] == m, cpos[0], float(TOPK * TOPK))
        for i in range(1, TOPK):
            p16 = jnp.minimum(p16, jnp.where(cand[i] == m, cpos[i], float(TOPK * TOPK)))
        am = jnp.min(p16, axis=0, keepdims=True)
        e16 = jnp.zeros((TOPK, ts), F32)
        for i in range(TOPK):
            hit = cpos[i] == am
            e16 = jnp.maximum(e16, jnp.where(hit, cid[i], 0.0))
            cand[i] = jnp.where(hit, -jnp.inf, cand[i])
        eid = jnp.max(e16, axis=0, keepdims=True)
        best = jnp.where(slot == r, m, best)
        bid = jnp.where(slot == r, eid, bid)

    e = jnp.exp(best - jnp.max(best, axis=0, keepdims=True))
    gate_ref[...] = e / jnp.sum(e, axis=0, keepdims=True)
    idx_ref[...] = bid.astype(I32)


def _route(h2d, g2, wq, keys):
    t, d = h2d.shape
    ts = min(ROUTE_TOKENS, t)
    assert t % ts == 0
    dq = wq.shape[1] // PEER_HEADS
    return pl.pallas_call(
        _route_kernel,
        grid=(t // ts, PEER_HEADS),
        in_specs=[
            pl.BlockSpec((ts, d), lambda i, h: (i, 0)),
            pl.BlockSpec((1, d), lambda i, h: (0, 0)),
            pl.BlockSpec((d, dq), lambda i, h: (0, h)),
            pl.BlockSpec((2, 1, N_KEYS, dq // 2), lambda i, h: (0, h, 0, 0)),
        ],
        out_specs=[
            pl.BlockSpec((ts, d), lambda i, h: (i, 0)),
            pl.BlockSpec((TOPK, ts), lambda i, h: (h, i)),
            pl.BlockSpec((TOPK, ts), lambda i, h: (h, i)),
        ],
        out_shape=[
            jax.ShapeDtypeStruct((t, d), F32),
            jax.ShapeDtypeStruct((PEER_HEADS * TOPK, t), I32),
            jax.ShapeDtypeStruct((PEER_HEADS * TOPK, t), F32),
        ],
        scratch_shapes=[pltpu.VMEM((ts, d), BF16)],
        compiler_params=pltpu.CompilerParams(
            dimension_semantics=("arbitrary", "arbitrary"),
            vmem_limit_bytes=VMEM_LIMIT),
        name="route",
    )(h2d, g2.reshape(1, d), wq.astype(BF16), keys.astype(BF16))


def _pack_table(tab):
    n, d = tab.shape
    assert d == SUBLANES * LANES
    tb = lax.bitcast_convert_type(tab.astype(BF16), jnp.uint16).astype(jnp.uint32)
    tb = tb.reshape(n, HALF_ROWS, 2, LANES)
    word = tb[:, :, 0, :] | (tb[:, :, 1, :] << 16)
    return lax.bitcast_convert_type(word, I32)


def _split_bf16(v):
    hi = v.astype(BF16).astype(F32)
    return hi, v - hi


def _gather_rows(idx_ref, tab_ref, t, rows_ref, nsel):
    for k in range(nsel):
        rows_ref[k * HALF_ROWS:(k + 1) * HALF_ROWS, :] = tab_ref[idx_ref[t, k]]


def _expert_in_kernel(idx_ref, x_ref, tab_ref, gate_ref, w_ref, rows0_ref, rows1_ref,
                      xs_ref, d_ref, *, tb, nsel):
    ncol = nsel * SUBLANES
    diag = (lax.broadcasted_iota(I32, (SUBLANES, ncol), 1) % SUBLANES
            == lax.broadcasted_iota(I32, (SUBLANES, ncol), 0))
    xh, xl = _split_bf16(x_ref[...])
    xs_ref[:, 0:SUBLANES, :] = xh.astype(BF16)
    xs_ref[:, SUBLANES:2 * SUBLANES, :] = xl.astype(BF16)

    def reduce(t, rows_ref):
        gb = pltpu.bitcast(rows_ref[...], BF16)
        r = lax.dot_general(xs_ref[t], gb, (((1,), (1,)), ((), ())),
                            preferred_element_type=F32)
        d = jnp.where(diag, r[0:SUBLANES] + r[SUBLANES:2 * SUBLANES], 0.0)
        d_ref[pl.ds(t, 1), :] = jnp.sum(d, axis=0, keepdims=True)

    _gather_rows(idx_ref, tab_ref, 0, rows0_ref, nsel)

    def pair(i, carry):
        t0 = 2 * i
        _gather_rows(idx_ref, tab_ref, t0 + 1, rows1_ref, nsel)
        reduce(t0, rows0_ref)
        _gather_rows(idx_ref, tab_ref, jnp.minimum(t0 + 2, tb - 1), rows0_ref, nsel)
        reduce(t0 + 1, rows1_ref)
        return carry

    lax.fori_loop(0, tb // 2, pair, 0)

    dh, dl = _split_bf16(d_ref[...])
    fold = (lax.broadcasted_iota(I32, (ncol, nsel), 0) // SUBLANES
            == lax.broadcasted_iota(I32, (ncol, nsel), 1))
    fold = jnp.where(fold, 1.0, 0.0).astype(BF16)
    dots = (jnp.dot(dh.astype(BF16), fold, preferred_element_type=F32)
            + jnp.dot(dl.astype(BF16), fold, preferred_element_type=F32))
    w_ref[...] = gate_ref[...] * _gelu_tanh(dots)


def _expert_in(idx, x3, tab, gate):
    t, nsel = idx.shape
    tb = min(EXPERT_TOKENS, t)
    assert t % tb == 0 and tb % 2 == 0
    return pl.pallas_call(
        functools.partial(_expert_in_kernel, tb=tb, nsel=nsel),
        grid=(t // tb,),
        in_specs=[
            pl.BlockSpec((tb, nsel), lambda i: (i, 0), memory_space=pltpu.SMEM),
            pl.BlockSpec((tb, SUBLANES, LANES), lambda i: (i, 0, 0)),
            pl.BlockSpec(tab.shape, lambda i: (0, 0, 0), pipeline_mode=pl.Buffered(1)),
            pl.BlockSpec((tb, nsel), lambda i: (i, 0)),
        ],
        out_specs=pl.BlockSpec((tb, nsel), lambda i: (i, 0)),
        out_shape=jax.ShapeDtypeStruct((t, nsel), F32),
        scratch_shapes=[
            pltpu.VMEM((nsel * HALF_ROWS, LANES), I32),
            pltpu.VMEM((nsel * HALF_ROWS, LANES), I32),
            pltpu.VMEM((tb, 2 * SUBLANES, LANES), BF16),
            pltpu.VMEM((tb, nsel * SUBLANES), F32),
        ],
        compiler_params=pltpu.CompilerParams(
            dimension_semantics=("arbitrary",),
            vmem_limit_bytes=VMEM_LIMIT),
        name="expert_in",
    )(idx, x3, tab, gate)


def _expert_out_kernel(idx_ref, w_ref, tab_ref, h_ref, g_ref, o_ref, rows0_ref, rows1_ref,
                       wh_ref, wl_ref, peer_ref, *, tb, nsel):
    ncol = nsel * SUBLANES
    diag = (lax.broadcasted_iota(I32, (SUBLANES, ncol), 1) % SUBLANES
            == lax.broadcasted_iota(I32, (SUBLANES, ncol), 0))
    spread = (lax.broadcasted_iota(I32, (nsel, ncol), 1) // SUBLANES
              == lax.broadcasted_iota(I32, (nsel, ncol), 0))
    spread = jnp.where(spread, 1.0, 0.0).astype(BF16)
    wh, wl = _split_bf16(w_ref[...])
    wh_ref[...] = jnp.dot(wh.astype(BF16), spread, preferred_element_type=F32)
    wl_ref[...] = jnp.dot(wl.astype(BF16), spread, preferred_element_type=F32)

    def combine(t, rows_ref):
        rh = jnp.broadcast_to(wh_ref[pl.ds(t, 1), :], (SUBLANES, ncol))
        rl = jnp.broadcast_to(wl_ref[pl.ds(t, 1), :], (SUBLANES, ncol))
        lhs = jnp.concatenate([jnp.where(diag, rh, 0.0), jnp.where(diag, rl, 0.0)], axis=0)
        gb = pltpu.bitcast(rows_ref[...], BF16)
        r = jnp.dot(lhs.astype(BF16), gb, preferred_element_type=F32)
        peer_ref[t] = r[0:SUBLANES] + r[SUBLANES:2 * SUBLANES]

    _gather_rows(idx_ref, tab_ref, 0, rows0_ref, nsel)

    def pair(i, carry):
        t0 = 2 * i
        _gather_rows(idx_ref, tab_ref, t0 + 1, rows1_ref, nsel)
        combine(t0, rows0_ref)
        _gather_rows(idx_ref, tab_ref, jnp.minimum(t0 + 2, tb - 1), rows0_ref, nsel)
        combine(t0 + 1, rows1_ref)
        return carry

    lax.fori_loop(0, tb // 2, pair, 0)

    hh = h_ref[...] + peer_ref[...]
    sq = jnp.sum(hh * hh, axis=2, keepdims=True)
    ms = jnp.sum(sq, axis=1, keepdims=True) / (SUBLANES * LANES)
    o_ref[...] = hh * lax.rsqrt(ms + EPS) * g_ref[...]


def _expert_out(idx, w, tab, h3, g):
    t, nsel = idx.shape
    tb = min(EXPERT_TOKENS, t)
    assert t % tb == 0 and tb % 2 == 0
    return pl.pallas_call(
        functools.partial(_expert_out_kernel, tb=tb, nsel=nsel),
        grid=(t // tb,),
        in_specs=[
            pl.BlockSpec((tb, nsel), lambda i: (i, 0), memory_space=pltpu.SMEM),
            pl.BlockSpec((tb, nsel), lambda i: (i, 0)),
            pl.BlockSpec(tab.shape, lambda i: (0, 0, 0), pipeline_mode=pl.Buffered(1)),
            pl.BlockSpec((tb, SUBLANES, LANES), lambda i: (i, 0, 0)),
            pl.BlockSpec((1, SUBLANES, LANES), lambda i: (0, 0, 0)),
        ],
        out_specs=pl.BlockSpec((tb, SUBLANES, LANES), lambda i: (i, 0, 0)),
        out_shape=jax.ShapeDtypeStruct((t, SUBLANES, LANES), F32),
        scratch_shapes=[
            pltpu.VMEM((nsel * HALF_ROWS, LANES), I32),
            pltpu.VMEM((nsel * HALF_ROWS, LANES), I32),
            pltpu.VMEM((tb, nsel * SUBLANES), F32),
            pltpu.VMEM((tb, nsel * SUBLANES), F32),
            pltpu.VMEM((tb, SUBLANES, LANES), F32),
        ],
        compiler_params=pltpu.CompilerParams(
            dimension_semantics=("arbitrary",),
            vmem_limit_bytes=VMEM_LIMIT),
        name="expert_out",
    )(idx, w, tab, h3, g.reshape(1, SUBLANES, LANES))


def kernel(x, norm1_g, w_in, pool_w, pool_scale, sgu_norm_g, sgu_w, sgu_b, w_out,
           norm2_g, peer_wq, peer_keys, peer_u, peer_v, final_g):
    b, s, d = x.shape
    assert norm1_g.shape[0] == 1 and d == SUBLANES * LANES
    h = _mixer(x, norm1_g[0], w_in[0], pool_w[0], pool_scale[0], sgu_norm_g[0],
               sgu_w[0], sgu_b[0], w_out[0])
    h2d = h.reshape(b * s, d)
    xn, idx_t, gate_t = _route(h2d, norm2_g[0], peer_wq[0], peer_keys[0])
    idx = idx_t.T
    w = _expert_in(idx, xn.reshape(b * s, SUBLANES, LANES),
                   _pack_table(peer_u[0]), gate_t.T)
    out = _expert_out(idx, w, _pack_table(peer_v[0]),
                      h2d.reshape(b * s, SUBLANES, LANES), final_g)
    return out.reshape(b, s, d)
```
